```python
import jax, jax.numpy as jnp
from jax import lax
import numpy as np

D_MODEL = 1024
BATCH = 4
SEQ = 8192
DEPTH = 1
DEC_BATCH = 32
DEC_SEQ = 32
PAST_LEN = 2048

CHUNK = 64
HEAD_DIM = 64
A_HEADS = D_MODEL // (2 * HEAD_DIM)
A_KV_HEADS = max(1, A_HEADS // 4)
WINDOW = 128
A_PREV_CHUNKS = -(-WINDOW // CHUNK)
A_ROWS = A_PREV_CHUNKS * CHUNK
B_HEADS = D_MODEL // (2 * HEAD_DIM)
B_PREV_CHUNKS = 8
B_ROWS = B_PREV_CHUNKS * CHUNK
REL_CLIP = 128
ROT_DIM = HEAD_DIM // 4
ROPE_THETA = 500000.0
N_GROUPS = 4
EXPERTS_PER_GROUP = 8
TOP_K = 2
EXPERT_FF = D_MODEL // 4
RMS_EPS = 1e-6
MASK_VALUE = -1e30
A_Q = A_HEADS * HEAD_DIM
A_KV = A_KV_HEADS * HEAD_DIM
B_W = B_HEADS * HEAD_DIM
IN_SPLITS = (A_Q, A_Q + A_KV, A_Q + 2 * A_KV, A_Q + 2 * A_KV + B_W, A_Q + 2 * A_KV + 2 * B_W)
IN_WIDTH = A_Q + 2 * A_KV + 3 * B_W
MIX_WIDTH = A_Q + B_W

kernel_name = 'hymba_swa_sink_chunkband_hmoe_step'


def _rmsnorm(x, g):
    xf = x.astype(jnp.float32)
    y = xf * lax.rsqrt(jnp.mean(xf * xf, axis=-1, keepdims=True) + RMS_EPS)
    return (y * g.astype(jnp.float32)).astype(x.dtype)


def _partial_rope(x, pos):
    half = ROT_DIM // 2
    inv_freq = ROPE_THETA ** (-jnp.arange(half, dtype=jnp.float32) / half)
    ang = pos.astype(jnp.float32)[:, None] * inv_freq[None, :]
    cos = jnp.cos(ang)[None, :, None, :]
    sin = jnp.sin(ang)[None, :, None, :]
    xf = x.astype(jnp.float32)
    x1 = xf[..., :half]
    x2 = xf[..., half:ROT_DIM]
    out = jnp.concatenate([x1 * cos - x2 * sin, x2 * cos + x1 * sin, xf[..., ROT_DIM:]], axis=-1)
    return out.astype(x.dtype)


def _attend(q, k, v, q_pos, k_pos, n_prev, sinks, rel_bias):
    n, lq, h, d = q.shape
    hk = k.shape[2]
    g = h // hk
    qg = q.reshape(n, lq, hk, g, d)
    s = jnp.einsum('nqjgd,nsjd->njgqs', qg, k, preferred_element_type=jnp.float32) * (d ** -0.5)
    qc = (q_pos // CHUNK)[:, None]
    kc = (k_pos // CHUNK)[None, :]
    valid = (k_pos[None, :] >= 0) & (kc <= qc) & (kc >= qc - n_prev)
    if rel_bias is not None:
        idx = jnp.clip(k_pos[None, :] - q_pos[:, None], -REL_CLIP, REL_CLIP) + REL_CLIP
        s = s + rel_bias[:, idx].astype(jnp.float32).reshape(hk, g, lq, -1)
    s = jnp.where(valid, s, MASK_VALUE)
    if sinks is not None:
        sk = sinks.astype(jnp.float32).reshape(hk, g, 1, 1)
        m = jnp.maximum(jnp.max(s, axis=-1, keepdims=True), sk)
        e = jnp.exp(s - m)
        p = e / (jnp.sum(e, axis=-1, keepdims=True) + jnp.exp(sk - m))
    else:
        p = jax.nn.softmax(s, axis=-1)
    o = jnp.einsum('njgqs,nsjd->nqjgd', p, v.astype(jnp.float32))
    return o.reshape(n, lq, h, d).astype(q.dtype)


def _chunk_band_attention(q, k, v, n_prev, sinks, rel_bias):
    n, s, h, d = q.shape
    pad = n_prev * CHUNK
    band = pad + CHUNK
    kp = jnp.pad(k, ((0, 0), (pad, 0), (0, 0), (0, 0)))
    vp = jnp.pad(v, ((0, 0), (pad, 0), (0, 0), (0, 0)))

    def one_chunk(c):
        start = c * CHUNK
        qc = lax.dynamic_slice_in_dim(q, start, CHUNK, axis=1)
        kc = lax.dynamic_slice_in_dim(kp, start, band, axis=1)
        vc = lax.dynamic_slice_in_dim(vp, start, band, axis=1)
        q_pos = start + jnp.arange(CHUNK)
        k_pos = start - pad + jnp.arange(band)
        return _attend(qc, kc, vc, q_pos, k_pos, n_prev, sinks, rel_bias)

    out = lax.map(one_chunk, jnp.arange(s // CHUNK))
    return jnp.moveaxis(out, 0, 1).reshape(n, s, h, d)


def _project(x, pos, norm1, w_in, q_norm_a, k_norm_a, q_norm_b, k_norm_b):
    n, l, _ = x.shape
    hx = _rmsnorm(x, norm1)
    z = jnp.einsum('nld,de->nle', hx, w_in)
    qa, ka, va, qb, kb, vb = jnp.split(z, IN_SPLITS, axis=-1)
    qa = qa.reshape(n, l, A_HEADS, HEAD_DIM)
    ka = ka.reshape(n, l, A_KV_HEADS, HEAD_DIM)
    va = va.reshape(n, l, A_KV_HEADS, HEAD_DIM)
    qb = qb.reshape(n, l, B_HEADS, HEAD_DIM)
    kb = kb.reshape(n, l, B_HEADS, HEAD_DIM)
    vb = vb.reshape(n, l, B_HEADS, HEAD_DIM)
    qa = _partial_rope(_rmsnorm(qa, q_norm_a), pos)
    ka = _partial_rope(_rmsnorm(ka, k_norm_a), pos)
    qb = _rmsnorm(qb, q_norm_b)
    kb = _rmsnorm(kb, k_norm_b)
    return qa, ka, va, qb, kb, vb


def _hier_moe(x, w_rg, b_rg, w_re, b_re, w_gate, w_up, w_down):
    n, l, dm = x.shape
    t = x.reshape(n * l, dm)
    g_logits = jnp.einsum('td,dg->tg', t, w_rg, preferred_element_type=jnp.float32) + b_rg.astype(jnp.float32)
    g_prob = jax.nn.softmax(g_logits, axis=-1)
    g_top_p, g_top = lax.top_k(g_prob, 1)
    e_logits = jnp.einsum('td,gde->tge', t, w_re, preferred_element_type=jnp.float32) + b_re.astype(jnp.float32)
    e_sel = jnp.take_along_axis(e_logits, g_top[:, :, None], axis=1)[:, 0]
    e_top_v, e_top = lax.top_k(e_sel, TOP_K)
    e_w = jax.nn.softmax(e_top_v, axis=-1) * g_top_p
    gate_in_group = jnp.einsum('tk,tke->te', e_w, jax.nn.one_hot(e_top, EXPERTS_PER_GROUP, dtype=jnp.float32))
    gate = jax.nn.one_hot(g_top[:, 0], N_GROUPS, dtype=jnp.float32)[:, :, None] * gate_in_group[:, None, :]
    out = jnp.zeros((n * l, dm), jnp.float32)
    for gi in range(N_GROUPS):
        for ei in range(EXPERTS_PER_GROUP):
            a = t @ w_gate[gi, ei]
            b = t @ w_up[gi, ei]
            hdn = jax.nn.silu(a) * b * gate[:, gi, ei, None].astype(t.dtype)
            out = out + jnp.matmul(hdn, w_down[gi, ei], preferred_element_type=jnp.float32)
    return out.astype(x.dtype).reshape(n, l, dm)


def _merge(x, oa, ob, out_norm_a, out_norm_b, w_out, norm2, w_rg, b_rg, w_re, b_re, w_gate, w_up, w_down):
    n, l, _ = x.shape
    oa = _rmsnorm(oa.reshape(n, l, A_Q), out_norm_a)
    ob = _rmsnorm(ob.reshape(n, l, B_W), out_norm_b)
    x = x + jnp.einsum('nle,ed->nld', jnp.concatenate([oa, ob], axis=-1), w_out)
    return x + _hier_moe(_rmsnorm(x, norm2), w_rg, b_rg, w_re, b_re, w_gate, w_up, w_down)


def _layer_prompt(x, proj, attn, post):
    _, s, _ = x.shape
    pos = jnp.arange(s)
    qa, ka, va, qb, kb, vb = _project(x, pos, *proj)
    sinks, rel_bias = attn
    oa = _chunk_band_attention(qa, ka, va, A_PREV_CHUNKS, sinks, None)
    ob = _chunk_band_attention(qb, kb, vb, B_PREV_CHUNKS, None, rel_bias)
    y = _merge(x, oa, ob, *post)
    return y, (ka[:, -A_ROWS:], va[:, -A_ROWS:], kb[:, -B_ROWS:], vb[:, -B_ROWS:])


def _layer_sample(x, ca_k, ca_v, cb_k, cb_v, proj, attn, post):
    _, l, _ = x.shape
    q_pos = PAST_LEN + jnp.arange(l)
    qa, ka, va, qb, kb, vb = _project(x, q_pos, *proj)
    sinks, rel_bias = attn
    wa = ca_k.shape[1]
    wb = cb_k.shape[1]
    oa = _attend(qa, jnp.concatenate([ca_k, ka], axis=1), jnp.concatenate([ca_v, va], axis=1),
                 q_pos, PAST_LEN - wa + jnp.arange(wa + l), A_PREV_CHUNKS, sinks, None)
    ob = _attend(qb, jnp.concatenate([cb_k, kb], axis=1), jnp.concatenate([cb_v, vb], axis=1),
                 q_pos, PAST_LEN - wb + jnp.arange(wb + l), B_PREV_CHUNKS, None, rel_bias)
    y = _merge(x, oa, ob, *post)
    return y, (ka, va, kb, vb)


def setup_inputs(seed: int = 0) -> dict:
    key = jax.random.key(seed)
    ks = jax.random.split(key, 26)
    f32 = jnp.float32

    def nrm(k, shape, scale=1.0):
        return jax.random.normal(k, shape, f32) * scale

    a_rows = min(A_ROWS, PAST_LEN)
    b_rows = min(B_ROWS, PAST_LEN)
    return {
        'x_prompt': nrm(ks[0], (BATCH, SEQ, D_MODEL)),
        'x_sample': nrm(ks[1], (DEC_BATCH, DEC_SEQ, D_MODEL)),
        'cache_a_k': nrm(ks[2], (DEPTH, DEC_BATCH, a_rows, A_KV_HEADS, HEAD_DIM)),
        'cache_a_v': nrm(ks[3], (DEPTH, DEC_BATCH, a_rows, A_KV_HEADS, HEAD_DIM)),
        'cache_b_k': nrm(ks[4], (DEPTH, DEC_BATCH, b_rows, B_HEADS, HEAD_DIM)),
        'cache_b_v': nrm(ks[5], (DEPTH, DEC_BATCH, b_rows, B_HEADS, HEAD_DIM)),
        'norm1': 1.0 + nrm(ks[6], (DEPTH, D_MODEL), 0.02),
        'w_in': nrm(ks[7], (DEPTH, D_MODEL, IN_WIDTH), D_MODEL ** -0.5),
        'q_norm_a': 1.0 + nrm(ks[8], (DEPTH, HEAD_DIM), 0.02),
        'k_norm_a': 1.0 + nrm(ks[9], (DEPTH, HEAD_DIM), 0.02),
        'q_norm_b': 1.0 + nrm(ks[10], (DEPTH, HEAD_DIM), 0.02),
        'k_norm_b': 1.0 + nrm(ks[11], (DEPTH, HEAD_DIM), 0.02),
        'sinks_a': nrm(ks[12], (DEPTH, A_HEADS), 1.0),
        'rel_bias_b': nrm(ks[13], (DEPTH, B_HEADS, 2 * REL_CLIP + 1), 0.5),
        'out_norm_a': 1.0 + nrm(ks[14], (DEPTH, A_Q), 0.02),
        'out_norm_b': 1.0 + nrm(ks[15], (DEPTH, B_W), 0.02),
        'w_out': nrm(ks[16], (DEPTH, MIX_WIDTH, D_MODEL), MIX_WIDTH ** -0.5),
        'norm2': 1.0 + nrm(ks[17], (DEPTH, D_MODEL), 0.02),
        'w_router_group': nrm(ks[18], (DEPTH, D_MODEL, N_GROUPS), D_MODEL ** -0.5),
        'b_router_group': nrm(ks[19], (DEPTH, N_GROUPS), 0.01),
        'w_router_expert': nrm(ks[20], (DEPTH, N_GROUPS, D_MODEL, EXPERTS_PER_GROUP), D_MODEL ** -0.5),
        'b_router_expert': nrm(ks[21], (DEPTH, N_GROUPS, EXPERTS_PER_GROUP), 0.01),
        'w_gate': nrm(ks[22], (DEPTH, N_GROUPS, EXPERTS_PER_GROUP, D_MODEL, EXPERT_FF), D_MODEL ** -0.5),
        'w_up': nrm(ks[23], (DEPTH, N_GROUPS, EXPERTS_PER_GROUP, D_MODEL, EXPERT_FF), D_MODEL ** -0.5),
        'w_down': nrm(ks[24], (DEPTH, N_GROUPS, EXPERTS_PER_GROUP, EXPERT_FF, D_MODEL), EXPERT_FF ** -0.5),
    }


def reference(x_prompt, x_sample, cache_a_k, cache_a_v, cache_b_k, cache_b_v,
              norm1, w_in, q_norm_a, k_norm_a, q_norm_b, k_norm_b, sinks_a, rel_bias_b,
              out_norm_a, out_norm_b, w_out, norm2, w_router_group, b_router_group,
              w_router_expert, b_router_expert, w_gate, w_up, w_down):
    y_p = x_prompt
    y_s = x_sample
    ak_p, av_p, bk_p, bv_p = [], [], [], []
    ak_s, av_s, bk_s, bv_s = [], [], [], []
    for layer in range(DEPTH):
        proj = (norm1[layer], w_in[layer], q_norm_a[layer], k_norm_a[layer], q_norm_b[layer], k_norm_b[layer])
        attn = (sinks_a[layer], rel_bias_b[layer])
        post = (out_norm_a[layer], out_norm_b[layer], w_out[layer], norm2[layer],
                w_router_group[layer], b_router_group[layer], w_router_expert[layer], b_router_expert[layer],
                w_gate[layer], w_up[layer], w_down[layer])
        y_p, st_p = _layer_prompt(y_p, proj, attn, post)
        y_s, st_s = _layer_sample(y_s, cache_a_k[layer], cache_a_v[layer], cache_b_k[layer], cache_b_v[layer],
                                  proj, attn, post)
        ak_p.append(st_p[0]); av_p.append(st_p[1]); bk_p.append(st_p[2]); bv_p.append(st_p[3])
        ak_s.append(st_s[0]); av_s.append(st_s[1]); bk_s.append(st_s[2]); bv_s.append(st_s[3])
    a_k_prompt = jnp.stack(ak_p, axis=0)
    a_v_prompt = jnp.stack(av_p, axis=0)
    b_k_prompt = jnp.stack(bk_p, axis=0)
    b_v_prompt = jnp.stack(bv_p, axis=0)
    a_k_sample = jnp.stack(ak_s, axis=0)
    a_v_sample = jnp.stack(av_s, axis=0)
    b_k_sample = jnp.stack(bk_s, axis=0)
    b_v_sample = jnp.stack(bv_s, axis=0)
    return (y_p, y_s, a_k_prompt, a_v_prompt, b_k_prompt, b_v_prompt,
            a_k_sample, a_v_sample, b_k_sample, b_v_sample)
```

```python
import functools

import jax
import jax.numpy as jnp
import numpy as np
from jax import lax
from jax.experimental import pallas as pl
from jax.experimental.pallas import tpu as pltpu

F32 = jnp.float32
BF16 = jnp.bfloat16

D_MODEL = 1024
CHUNK = 64
HEAD_DIM = 64
A_HEADS = 8
A_KV_HEADS = 2
A_PREV_CHUNKS = 2
A_ROWS = A_PREV_CHUNKS * CHUNK
B_HEADS = 8
B_PREV_CHUNKS = 8
B_ROWS = B_PREV_CHUNKS * CHUNK
REL_CLIP = 128
ROT_DIM = HEAD_DIM // 4
ROPE_THETA = 500000.0
N_GROUPS = 4
EXPERTS_PER_GROUP = 8
N_EXPERTS = N_GROUPS * EXPERTS_PER_GROUP
EXPERT_FF = D_MODEL // 4
RMS_EPS = 1e-6
MASK_VALUE = -1e30
PAST_LEN = 2048
A_Q = A_HEADS * HEAD_DIM
A_KV = A_KV_HEADS * HEAD_DIM
B_W = B_HEADS * HEAD_DIM

LANES = 128
ROW_TILE = 512
MOE_ROW_TILE = 1024
VMEM_LIMIT = 56 * 1024 * 1024
A_HEAD_ORDER = (0, 4, 1, 5, 2, 6, 3, 7)

QA0, QB0, KB0, KA0, VA0, VB0 = 0, 512, 1024, 1536, 1664, 1792
NORM_W = 1664
ROUTE_GROUP0 = 0
ROUTE_EXPERT0 = N_GROUPS


def _rms(x, gain):
    return x * lax.rsqrt(jnp.mean(x * x, axis=-1, keepdims=True) + RMS_EPS) * gain


def _proj_body(x_ref, g1_ref, w_ref, gn_ref, c_ref, sp_ref, sm_ref, bd_ref,
               qa_ref, qb_ref, kb_ref, ka_ref, va_ref, vb_ref,
               cka_ref, cva_ref, ckb_ref, cvb_ref, *, cache_all, tiles_per_seq):
    h = _rms(x_ref[...], g1_ref[...]).astype(BF16)
    bd = bd_ref[...]

    def seg(lo, width):
        return jnp.dot(h, w_ref[:, lo:lo + width], preferred_element_type=F32)

    def head_norm(z, lo):
        width = z.shape[1]
        zz = (z * z).astype(BF16)
        parts = []
        for c in range(0, width, 256):
            w_ = min(256, width - c)
            parts.append(jnp.dot(zz[:, c:c + w_], bd[:w_, :w_], preferred_element_type=F32))
        ss = parts[0] if len(parts) == 1 else jnp.concatenate(parts, axis=1)
        return z * lax.rsqrt(ss * (1.0 / HEAD_DIM) + RMS_EPS) * gn_ref[:, lo:lo + width]

    def rope(z):
        c, sp, sm = c_ref[...], sp_ref[...], sm_ref[...]
        blocks = []
        for b in range(z.shape[1] // LANES):
            blk = z[:, b * LANES:(b + 1) * LANES]
            blocks.append(blk * c + pltpu.roll(blk, ROT_DIM // 2, 1) * sp
                          + pltpu.roll(blk, LANES - ROT_DIM // 2, 1) * sm)
        return blocks[0] if len(blocks) == 1 else jnp.concatenate(blocks, axis=1)

    qa_ref[...] = rope(head_norm(seg(QA0, A_Q), QA0)).astype(BF16)
    qb_ref[...] = head_norm(seg(QB0, B_W), QB0).astype(BF16)
    kb = head_norm(seg(KB0, B_W), KB0)
    kb_ref[...] = kb.astype(BF16)
    ka = rope(head_norm(seg(KA0, A_KV), KA0))
    ka_ref[...] = ka.astype(BF16)
    va = seg(VA0, A_KV)
    va_ref[...] = va.astype(BF16)
    vb = seg(VB0, B_W)
    vb_ref[...] = vb.astype(BF16)

    def write_cache():
        a_rows = cka_ref.shape[0]
        b_rows = ckb_ref.shape[0]
        cka_ref[...] = ka[ka.shape[0] - a_rows:, :]
        cva_ref[...] = va[va.shape[0] - a_rows:, :]
        ckb_ref[...] = kb[kb.shape[0] - b_rows:, :]
        cvb_ref[...] = vb[vb.shape[0] - b_rows:, :]

    if cache_all:
        write_cache()
    else:
        pl.when(pl.program_id(0) % tiles_per_seq == tiles_per_seq - 1)(write_cache)


def _project(x2d, params, tables, *, tm, tiles_per_seq, cache_all, n_seq):
    t_rows = x2d.shape[0]
    grid = (t_rows // tm,)
    row = lambda w: pl.BlockSpec((tm, w), lambda i: (i, 0))
    const = lambda a: pl.BlockSpec(a.shape, lambda i: (0,) * a.ndim)
    tab = pl.BlockSpec((tm, LANES), lambda i: (i % tiles_per_seq, 0))
    if cache_all:
        ca_spec, cb_spec = row(A_KV), row(B_W)
        ca_rows, cb_rows = t_rows, t_rows
    else:
        ca_spec = pl.BlockSpec((A_ROWS, A_KV), lambda i: (i // tiles_per_seq, 0))
        cb_spec = pl.BlockSpec((B_ROWS, B_W), lambda i: (i // tiles_per_seq, 0))
        ca_rows, cb_rows = n_seq * A_ROWS, n_seq * B_ROWS
    out_shape = (
        jax.ShapeDtypeStruct((t_rows, A_Q), BF16), jax.ShapeDtypeStruct((t_rows, B_W), BF16),
        jax.ShapeDtypeStruct((t_rows, B_W), BF16), jax.ShapeDtypeStruct((t_rows, A_KV), BF16),
        jax.ShapeDtypeStruct((t_rows, A_KV), BF16), jax.ShapeDtypeStruct((t_rows, B_W), BF16),
        jax.ShapeDtypeStruct((ca_rows, A_KV), F32), jax.ShapeDtypeStruct((ca_rows, A_KV), F32),
        jax.ShapeDtypeStruct((cb_rows, B_W), F32), jax.ShapeDtypeStruct((cb_rows, B_W), F32),
    )
    out_specs = (row(A_Q), row(B_W), row(B_W), row(A_KV), row(A_KV), row(B_W),
                 ca_spec, ca_spec, cb_spec, cb_spec)
    g1, w, gn, bd = params
    c, sp, sm = tables
    return pl.pallas_call(
        functools.partial(_proj_body, cache_all=cache_all, tiles_per_seq=tiles_per_seq),
        out_shape=out_shape, grid=grid,
        in_specs=[row(D_MODEL), const(g1), const(w), const(gn), tab, tab, tab, const(bd)],
        out_specs=out_specs,
        compiler_params=pltpu.CompilerParams(dimension_semantics=("arbitrary",),
                                             vmem_limit_bytes=VMEM_LIMIT),
        name="proj",
    )(x2d, g1, w, gn, c, sp, sm, bd)


def _stack_pairs(q, n_blocks):
    rows = q.shape[0]
    lo = lax.broadcasted_iota(jnp.int32, (rows, LANES), 1) < HEAD_DIM
    zero = jnp.zeros((rows, LANES), q.dtype)
    parts = []
    for b in range(n_blocks):
        q2 = q[:, b * LANES:(b + 1) * LANES]
        parts += [jnp.where(lo, q2, zero), jnp.where(lo, zero, q2)]
    return jnp.concatenate(parts, axis=0)


def _unstack_pairs(o, rows, n_blocks):
    lo = lax.broadcasted_iota(jnp.int32, (rows, LANES), 1) < HEAD_DIM
    blocks = []
    for b in range(n_blocks):
        top = o[(2 * b) * rows:(2 * b + 1) * rows]
        bot = o[(2 * b + 1) * rows:(2 * b + 2) * rows]
        blocks.append(jnp.where(lo, top, bot))
    return blocks[0] if n_blocks == 1 else jnp.concatenate(blocks, axis=1)


def _softmax_pv(qm, kband, vband, *, bias=None, sink=None, valid_from=None):
    s = lax.dot_general(qm, kband, (((1,), (1,)), ((), ())), preferred_element_type=F32)
    if bias is not None:
        s = s + bias
    if valid_from is not None:
        col = lax.broadcasted_iota(jnp.int32, s.shape, 1)
        s = jnp.where(col >= valid_from, s, MASK_VALUE)
    m = jnp.max(s, axis=1, keepdims=True)
    if sink is not None:
        m = jnp.maximum(m, sink)
    e = jnp.exp(s - m)
    l = jnp.sum(e, axis=1, keepdims=True)
    if sink is not None:
        l = l + jnp.exp(sink - m)
    return jnp.dot(e.astype(BF16), vband, preferred_element_type=F32) / l


def _route(logits):
    lane = lax.broadcasted_iota(jnp.int32, logits.shape, 1).astype(F32)
    neg = MASK_VALUE
    gmask = lane < float(N_GROUPS)
    gl = jnp.where(gmask, logits, neg)
    gmax = jnp.max(gl, axis=1, keepdims=True)
    gsum = jnp.sum(jnp.exp(gl - gmax), axis=1, keepdims=True)
    g_top_p = 1.0 / gsum
    g_top = jnp.min(jnp.where(gl == gmax, lane, float(LANES)), axis=1, keepdims=True)
    e_lo = float(ROUTE_EXPERT0) + float(EXPERTS_PER_GROUP) * g_top
    emask = (lane >= e_lo) & (lane < e_lo + float(EXPERTS_PER_GROUP))
    el = jnp.where(emask, logits, neg)
    v1 = jnp.max(el, axis=1, keepdims=True)
    i1 = jnp.min(jnp.where(el == v1, lane, float(LANES)), axis=1, keepdims=True)
    el2 = jnp.where(lane == i1, neg, el)
    v2 = jnp.max(el2, axis=1, keepdims=True)
    i2 = jnp.min(jnp.where(el2 == v2, lane, float(LANES)), axis=1, keepdims=True)
    t = jnp.exp(v2 - v1)
    den = 1.0 + t
    w1 = (1.0 / den) * g_top_p
    w2 = (t / den) * g_top_p
    return jnp.where(lane == i1, w1, 0.0) + jnp.where(lane == i2, w2, 0.0)


def _merge_route(o_all, x, wout_ref, g2_ref, wr_ref, br_ref, x1_ref, xn_ref, gate_ref):
    x1 = x + jnp.dot(o_all, wout_ref[...], preferred_element_type=F32)
    x1_ref[...] = x1
    xn = _rms(x1, g2_ref[...]).astype(BF16)
    xn_ref[...] = xn
    logits = jnp.dot(xn, wr_ref[...], preferred_element_type=F32) + br_ref[...]
    gate_ref[...] = _route(logits)


def _attn_prompt_body(qa_ref, qb_ref, kac_ref, kap_ref, vac_ref, vap_ref,
                      kbc_ref, kbp_ref, vbc_ref, vbp_ref,
                      x_ref, bias_ref, sink_ref, ga_ref, gb_ref, wout_ref, g2_ref, wr_ref, br_ref,
                      x1_ref, xn_ref, gate_ref,
                      ka_s, va_s, kb_s, vb_s, o_s):
    tm = qa_ref.shape[0]
    ka_s[0:A_ROWS] = kap_ref[...]
    ka_s[A_ROWS:A_ROWS + tm] = kac_ref[...]
    va_s[0:A_ROWS] = vap_ref[...]
    va_s[A_ROWS:A_ROWS + tm] = vac_ref[...]
    kb_s[0:B_ROWS] = kbp_ref[...]
    kb_s[B_ROWS:B_ROWS + tm] = kbc_ref[...]
    vb_s[0:B_ROWS] = vbp_ref[...]
    vb_s[B_ROWS:B_ROWS + tm] = vbc_ref[...]
    sink = sink_ref[:, 0:1]

    def chunk(i, masked):
        r0 = pl.multiple_of(i * CHUNK, CHUNK)
        qm = _stack_pairs(qa_ref[pl.ds(r0, CHUNK), :], A_Q // LANES)
        o = _softmax_pv(qm, ka_s[pl.ds(r0, A_ROWS + CHUNK), :], va_s[pl.ds(r0, A_ROWS + CHUNK), :],
                        sink=sink, valid_from=(A_ROWS - CHUNK * i) if masked else None)
        oa = _unstack_pairs(o, CHUNK, A_Q // LANES)
        o_s[pl.ds(r0, CHUNK), 0:A_Q] = _rms(oa, ga_ref[...]).astype(BF16)
        qb_c = qb_ref[pl.ds(r0, CHUNK), :]
        blocks = []
        for p in range(B_W // LANES):
            qm = _stack_pairs(qb_c[:, p * LANES:(p + 1) * LANES], 1)
            o = _softmax_pv(qm, kb_s[pl.ds(r0, B_ROWS + CHUNK), p * LANES:(p + 1) * LANES],
                            vb_s[pl.ds(r0, B_ROWS + CHUNK), p * LANES:(p + 1) * LANES],
                            bias=bias_ref[p], valid_from=(B_ROWS - CHUNK * i) if masked else None)
            blocks.append(_unstack_pairs(o, CHUNK, 1))
        ob = jnp.concatenate(blocks, axis=1)
        o_s[pl.ds(r0, CHUNK), A_Q:A_Q + B_W] = _rms(ob, gb_ref[...]).astype(BF16)

    def run(masked):
        def body(i, carry):
            chunk(i, masked)
            return carry
        lax.fori_loop(0, tm // CHUNK, body, 0)

    first = pl.program_id(1) == 0
    pl.when(first)(lambda: run(True))
    pl.when(jnp.logical_not(first))(lambda: run(False))

    _merge_route(o_s[...], x_ref[...], wout_ref, g2_ref, wr_ref, br_ref, x1_ref, xn_ref, gate_ref)


def _attn_prompt(proj_out, x2d, consts, *, n_batch, tiles_per_seq):
    qa, qb, kb, ka, va, vb = proj_out
    bias2, sink_tab, ga, gb, wout, g2, wr, br = consts
    tm = ROW_TILE
    t_rows = x2d.shape[0]
    tile = lambda n, j: n * tiles_per_seq + j
    cur = lambda w: pl.BlockSpec((tm, w), lambda n, j: (tile(n, j), 0))
    prev_a = pl.BlockSpec((A_ROWS, A_KV),
                          lambda n, j: (jnp.maximum(tile(n, j) * (tm // A_ROWS) - 1, 0), 0))
    prev_b = pl.BlockSpec((B_ROWS, B_W), lambda n, j: (jnp.maximum(tile(n, j) - 1, 0), 0))
    const = lambda a: pl.BlockSpec(a.shape, lambda n, j: (0,) * a.ndim)
    return pl.pallas_call(
        _attn_prompt_body,
        out_shape=(jax.ShapeDtypeStruct((t_rows, D_MODEL), F32),
                   jax.ShapeDtypeStruct((t_rows, D_MODEL), BF16),
                   jax.ShapeDtypeStruct((t_rows, LANES), F32)),
        grid=(n_batch, tiles_per_seq),
        in_specs=[cur(A_Q), cur(B_W), cur(A_KV), prev_a, cur(A_KV), prev_a,
                  cur(B_W), prev_b, cur(B_W), prev_b,
                  cur(D_MODEL), const(bias2), const(sink_tab), const(ga), const(gb),
                  const(wout), const(g2), const(wr), const(br)],
        out_specs=(cur(D_MODEL), cur(D_MODEL), cur(LANES)),
        scratch_shapes=[pltpu.VMEM((A_ROWS + tm, A_KV), BF16), pltpu.VMEM((A_ROWS + tm, A_KV), BF16),
                        pltpu.VMEM((B_ROWS + tm, B_W), BF16), pltpu.VMEM((B_ROWS + tm, B_W), BF16),
                        pltpu.VMEM((tm, D_MODEL), BF16)],
        compiler_params=pltpu.CompilerParams(dimension_semantics=("arbitrary", "arbitrary"),
                                             vmem_limit_bytes=VMEM_LIMIT),
        name="attn_prompt",
    )(qa, qb, ka, ka, va, va, kb, kb, vb, vb, x2d, bias2, sink_tab, ga, gb, wout, g2, wr, br)


def _attn_sample_body(qa_ref, qb_ref, kan_ref, van_ref, kbn_ref, vbn_ref,
                      cak_ref, cav_ref, cbk_ref, cbv_ref,
                      x_ref, bias_ref, sink_ref, ga_ref, gb_ref, wout_ref, g2_ref, wr_ref, br_ref,
                      x1_ref, xn_ref, gate_ref,
                      ka_s, va_s, kb_s, vb_s, o_s, *, seq):
    n_elem = cak_ref.shape[0]
    wa = cak_ref.shape[1]
    wb = cbk_ref.shape[1]
    sink = sink_ref[:, 0:1]

    def element(e, carry):
        r0 = pl.multiple_of(e * seq, seq)
        ka_s[0:wa] = cak_ref[e].astype(BF16)
        ka_s[wa:wa + seq] = kan_ref[pl.ds(r0, seq), :]
        va_s[0:wa] = cav_ref[e].astype(BF16)
        va_s[wa:wa + seq] = van_ref[pl.ds(r0, seq), :]
        kb_s[0:wb] = cbk_ref[e].astype(BF16)
        kb_s[wb:wb + seq] = kbn_ref[pl.ds(r0, seq), :]
        vb_s[0:wb] = cbv_ref[e].astype(BF16)
        vb_s[wb:wb + seq] = vbn_ref[pl.ds(r0, seq), :]

        qm = _stack_pairs(qa_ref[pl.ds(r0, seq), :], A_Q // LANES)
        o = _softmax_pv(qm, ka_s[...], va_s[...], sink=sink)
        oa = _unstack_pairs(o, seq, A_Q // LANES)
        o_s[pl.ds(r0, seq), 0:A_Q] = _rms(oa, ga_ref[...]).astype(BF16)

        qb_c = qb_ref[pl.ds(r0, seq), :]
        blocks = []
        for p in range(B_W // LANES):
            qm = _stack_pairs(qb_c[:, p * LANES:(p + 1) * LANES], 1)
            o = _softmax_pv(qm, kb_s[:, p * LANES:(p + 1) * LANES],
                            vb_s[:, p * LANES:(p + 1) * LANES], bias=bias_ref[p])
            blocks.append(_unstack_pairs(o, seq, 1))
        ob = jnp.concatenate(blocks, axis=1)
        o_s[pl.ds(r0, seq), A_Q:A_Q + B_W] = _rms(ob, gb_ref[...]).astype(BF16)
        return carry

    lax.fori_loop(0, n_elem, element, 0)
    _merge_route(o_s[...], x_ref[...], wout_ref, g2_ref, wr_ref, br_ref, x1_ref, xn_ref, gate_ref)


def _attn_sample(proj_out, caches, x2d, consts, *, n_batch, seq, elems_per_step):
    qa, qb, kb, ka, va, vb = proj_out
    cak, cav, cbk, cbv = caches
    bias_s, sink_tab, ga, gb, wout, g2, wr, br = consts
    wa, wb = cak.shape[1], cbk.shape[1]
    tm = elems_per_step * seq
    t_rows = x2d.shape[0]
    row = lambda w: pl.BlockSpec((tm, w), lambda i: (i, 0))
    cache = lambda a: pl.BlockSpec((elems_per_step,) + a.shape[1:], lambda i: (i, 0, 0))
    const = lambda a: pl.BlockSpec(a.shape, lambda i: (0,) * a.ndim)
    return pl.pallas_call(
        functools.partial(_attn_sample_body, seq=seq),
        out_shape=(jax.ShapeDtypeStruct((t_rows, D_MODEL), F32),
                   jax.ShapeDtypeStruct((t_rows, D_MODEL), BF16),
                   jax.ShapeDtypeStruct((t_rows, LANES), F32)),
        grid=(n_batch // elems_per_step,),
        in_specs=[row(A_Q), row(B_W), row(A_KV), row(A_KV), row(B_W), row(B_W),
                  cache(cak), cache(cav), cache(cbk), cache(cbv),
                  row(D_MODEL), const(bias_s), const(sink_tab), const(ga), const(gb),
                  const(wout), const(g2), const(wr), const(br)],
        out_specs=(row(D_MODEL), row(D_MODEL), row(LANES)),
        scratch_shapes=[pltpu.VMEM((wa + seq, A_KV), BF16), pltpu.VMEM((wa + seq, A_KV), BF16),
                        pltpu.VMEM((wb + seq, B_W), BF16), pltpu.VMEM((wb + seq, B_W), BF16),
                        pltpu.VMEM((tm, D_MODEL), BF16)],
        compiler_params=pltpu.CompilerParams(dimension_semantics=("arbitrary",),
                                             vmem_limit_bytes=VMEM_LIMIT),
        name="attn_sample",
    )(qa, qb, ka, va, kb, vb, cak, cav, cbk, cbv, x2d, bias_s, sink_tab, ga, gb, wout, g2, wr, br)


def _moe_body(xn_ref, gate_ref, x1_ref, wg_ref, wu_ref, wd_ref, y_ref, acc_ref):
    e = pl.program_id(1)

    @pl.when(e == 0)
    def _():
        acc_ref[...] = jnp.zeros_like(acc_ref)

    xn = xn_ref[...]
    a = jnp.dot(xn, wg_ref[...], preferred_element_type=F32)
    b = jnp.dot(xn, wu_ref[...], preferred_element_type=F32)
    gate = gate_ref[...]
    lane = lax.broadcasted_iota(jnp.int32, gate.shape, 1)
    g = jnp.sum(jnp.where(lane == e + ROUTE_EXPERT0, gate, 0.0), axis=1, keepdims=True)
    hdn = (jax.nn.silu(a) * b * g).astype(BF16)
    acc_ref[...] += jnp.dot(hdn, wd_ref[...], preferred_element_type=F32)

    @pl.when(e == pl.num_programs(1) - 1)
    def _():
        y_ref[...] = x1_ref[...] + acc_ref[...]


def _moe_dense(xn, gate, x1, wg, wu, wd):
    t_rows = xn.shape[0]
    tm = min(MOE_ROW_TILE, t_rows)
    row = lambda w: pl.BlockSpec((tm, w), lambda i, e: (i, 0))
    return pl.pallas_call(
        _moe_body,
        out_shape=jax.ShapeDtypeStruct((t_rows, D_MODEL), F32),
        grid=(t_rows // tm, N_EXPERTS),
        in_specs=[row(D_MODEL), row(LANES), row(D_MODEL),
                  pl.BlockSpec((None, D_MODEL, EXPERT_FF), lambda i, e: (e, 0, 0)),
                  pl.BlockSpec((None, D_MODEL, EXPERT_FF), lambda i, e: (e, 0, 0)),
                  pl.BlockSpec((None, EXPERT_FF, D_MODEL), lambda i, e: (e, 0, 0))],
        out_specs=row(D_MODEL),
        scratch_shapes=[pltpu.VMEM((tm, D_MODEL), F32)],
        compiler_params=pltpu.CompilerParams(dimension_semantics=("arbitrary", "arbitrary"),
                                             vmem_limit_bytes=VMEM_LIMIT),
        name="moe_dense",
    )(xn, gate, x1, wg, wu, wd)


def _rope_tables(pos):
    half = ROT_DIM // 2
    inv_freq = ROPE_THETA ** (-jnp.arange(half, dtype=F32) / half)
    ang = pos.astype(F32)[:, None] * inv_freq[None, :]
    cos, sin = jnp.cos(ang), jnp.sin(ang)
    n = pos.shape[0]
    rest = HEAD_DIM - ROT_DIM
    c = jnp.concatenate([cos, cos, jnp.ones((n, rest), F32)], axis=1)
    sp = jnp.concatenate([jnp.zeros((n, half), F32), sin, jnp.zeros((n, rest), F32)], axis=1)
    sm = jnp.concatenate([-sin, jnp.zeros((n, half + rest), F32)], axis=1)
    rep = LANES // HEAD_DIM
    return tuple(jnp.tile(t, (1, rep)) for t in (c, sp, sm))


def _pair_bias(rel_bias, rows, cols):
    i = jnp.arange(rows)[:, None]
    j = jnp.arange(cols)[None, :]
    idx = jnp.clip(j - B_ROWS - i, -REL_CLIP, REL_CLIP) + REL_CLIP
    full = rel_bias.astype(F32)[:, idx]
    return full.reshape(B_HEADS // 2, 2 * rows, cols)


def _sink_table(sinks, rows):
    order = jnp.asarray(A_HEAD_ORDER)
    per_row = jnp.repeat(sinks.astype(F32)[order], rows)
    return jnp.broadcast_to(per_row[:, None], (per_row.shape[0], LANES))


def kernel(x_prompt, x_sample, cache_a_k, cache_a_v, cache_b_k, cache_b_v, norm1, w_in, q_norm_a,
           k_norm_a, q_norm_b, k_norm_b, sinks_a, rel_bias_b, out_norm_a, out_norm_b, w_out, norm2,
           w_router_group, b_router_group, w_router_expert, b_router_expert, w_gate, w_up, w_down):
    depth = norm1.shape[0]
    assert depth == 1, "single layer"
    n_b, seq, _ = x_prompt.shape
    n_s, seq_s, _ = x_sample.shape
    assert seq % ROW_TILE == 0 and ROW_TILE % seq_s == 0 and (n_s * seq_s) % ROW_TILE == 0
    order = jnp.asarray(A_HEAD_ORDER)

    w = w_in[0]
    qa_w = w[:, 0:A_Q].reshape(D_MODEL, A_HEADS, HEAD_DIM)[:, order].reshape(D_MODEL, A_Q)
    o_qb = A_Q + 2 * A_KV
    w_perm = jnp.concatenate([qa_w, w[:, o_qb:o_qb + B_W], w[:, o_qb + B_W:o_qb + 2 * B_W],
                              w[:, A_Q:A_Q + A_KV], w[:, A_Q + A_KV:A_Q + 2 * A_KV],
                              w[:, o_qb + 2 * B_W:]], axis=1).astype(BF16)
    q_scale = HEAD_DIM ** -0.5
    gn = jnp.concatenate([jnp.tile(q_norm_a[0], A_HEADS) * q_scale, jnp.tile(q_norm_b[0], B_HEADS) * q_scale,
                          jnp.tile(k_norm_b[0], B_HEADS), jnp.tile(k_norm_a[0], A_KV_HEADS)])[None, :].astype(F32)
    g1 = norm1[0][None, :].astype(F32)
    r = np.arange(256)
    bd = jnp.asarray((r[:, None] // HEAD_DIM) == (r[None, :] // HEAD_DIM), dtype=BF16)
    proj_params = (g1, w_perm, gn, bd)

    ga = out_norm_a[0].reshape(A_HEADS, HEAD_DIM)[order].reshape(1, A_Q).astype(F32)
    gb = out_norm_b[0][None, :].astype(F32)
    wo = w_out[0]
    wout = jnp.concatenate([wo[:A_Q].reshape(A_HEADS, HEAD_DIM, D_MODEL)[order].reshape(A_Q, D_MODEL),
                            wo[A_Q:]], axis=0).astype(BF16)
    g2 = norm2[0][None, :].astype(F32)
    n_route = N_GROUPS + N_EXPERTS
    wr = jnp.concatenate([w_router_group[0],
                          jnp.transpose(w_router_expert[0], (1, 0, 2)).reshape(D_MODEL, N_EXPERTS),
                          jnp.zeros((D_MODEL, LANES - n_route), F32)], axis=1).astype(BF16)
    br = jnp.concatenate([b_router_group[0], b_router_expert[0].reshape(-1),
                          jnp.zeros((LANES - n_route,), F32)])[None, :].astype(F32)
    wg = w_gate[0].reshape(N_EXPERTS, D_MODEL, EXPERT_FF).astype(BF16)
    wu = w_up[0].reshape(N_EXPERTS, D_MODEL, EXPERT_FF).astype(BF16)
    wd = w_down[0].reshape(N_EXPERTS, EXPERT_FF, D_MODEL).astype(BF16)

    xp = x_prompt.reshape(n_b * seq, D_MODEL)
    tps = seq // ROW_TILE
    outs = _project(xp, proj_params, _rope_tables(jnp.arange(seq)), tm=ROW_TILE, tiles_per_seq=tps,
                    cache_all=False, n_seq=n_b)
    ak_p, av_p, bk_p, bv_p = outs[6:]
    consts = (_pair_bias(rel_bias_b[0], CHUNK, B_ROWS + CHUNK), _sink_table(sinks_a[0], CHUNK),
              ga, gb, wout, g2, wr, br)
    x1, xn, gate = _attn_prompt(outs[:6], xp, consts, n_batch=n_b, tiles_per_seq=tps)
    y_p = _moe_dense(xn, gate, x1, wg, wu, wd).reshape(n_b, seq, D_MODEL)

    xs = x_sample.reshape(n_s * seq_s, D_MODEL)
    pos_s = jnp.tile(PAST_LEN + jnp.arange(seq_s), ROW_TILE // seq_s)
    outs = _project(xs, proj_params, _rope_tables(pos_s), tm=ROW_TILE, tiles_per_seq=1,
                    cache_all=True, n_seq=n_s)
    ak_s, av_s, bk_s, bv_s = outs[6:]
    wa, wb = cache_a_k.shape[2], cache_b_k.shape[2]
    caches = (cache_a_k[0].reshape(n_s, wa, A_KV), cache_a_v[0].reshape(n_s, wa, A_KV),
              cache_b_k[0].reshape(n_s, wb, B_W), cache_b_v[0].reshape(n_s, wb, B_W))
    consts = (_pair_bias(rel_bias_b[0], seq_s, wb + seq_s), _sink_table(sinks_a[0], seq_s),
              ga, gb, wout, g2, wr, br)
    x1, xn, gate = _attn_sample(outs[:6], caches, xs, consts, n_batch=n_s, seq=seq_s, elems_per_step=4)
    y_s = _moe_dense(xn, gate, x1, wg, wu, wd).reshape(n_s, seq_s, D_MODEL)

    return (y_p, y_s,
            ak_p.reshape(1, n_b, A_ROWS, A_KV_HEADS, HEAD_DIM), av_p.reshape(1, n_b, A_ROWS, A_KV_HEADS, HEAD_DIM),
            bk_p.reshape(1, n_b, B_ROWS, B_HEADS, HEAD_DIM), bv_p.reshape(1, n_b, B_ROWS, B_HEADS, HEAD_DIM),
            ak_s.reshape(1, n_s, seq_s, A_KV_HEADS, HEAD_DIM), av_s.reshape(1, n_s, seq_s, A_KV_HEADS, HEAD_DIM),
            bk_s.reshape(1, n_s, seq_s, B_HEADS, HEAD_DIM), bv_s.reshape(1, n_s, seq_s, B_HEADS, HEAD_DIM))
```

```python
import functools

import jax
import jax.numpy as jnp
import numpy as np
from jax import lax
from jax.experimental import pallas as pl
from jax.experimental.pallas import tpu as pltpu

F32 = jnp.float32
BF16 = jnp.bfloat16

D_MODEL = 1024
CHUNK = 64
HEAD_DIM = 64
A_HEADS = 8
A_KV_HEADS = 2
A_PREV_CHUNKS = 2
A_ROWS = A_PREV_CHUNKS * CHUNK
B_HEADS = 8
B_PREV_CHUNKS = 8
B_ROWS = B_PREV_CHUNKS * CHUNK
REL_CLIP = 128
ROT_DIM = HEAD_DIM // 4
ROPE_THETA = 500000.0
N_GROUPS = 4
EXPERTS_PER_GROUP = 8
N_EXPERTS = N_GROUPS * EXPERTS_PER_GROUP
EXPERT_FF = D_MODEL // 4
RMS_EPS = 1e-6
MASK_VALUE = -1e30
PAST_LEN = 2048
A_Q = A_HEADS * HEAD_DIM
A_KV = A_KV_HEADS * HEAD_DIM
B_W = B_HEADS * HEAD_DIM

LANES = 128
ROW_TILE = 512
MOE_ROW_TILE = 256
VMEM_LIMIT = 56 * 1024 * 1024
PAIRS = tuple((lo, hi) for lo in range(EXPERTS_PER_GROUP) for hi in range(lo + 1, EXPERTS_PER_GROUP))
PAIRS_PER_GROUP = len(PAIRS)
N_BINS = N_GROUPS * PAIRS_PER_GROUP
HALF = D_MODEL // 2
REC_W = HALF + LANES
DRAIN_UNROLL = 64
A_HEAD_ORDER = (0, 4, 1, 5, 2, 6, 3, 7)

QA0, QB0, KB0, KA0, VA0, VB0 = 0, 512, 1024, 1536, 1664, 1792
NORM_W = 1664
ROUTE_GROUP0 = 0
ROUTE_EXPERT0 = N_GROUPS


def _rms(x, gain):
    return x * lax.rsqrt(jnp.mean(x * x, axis=-1, keepdims=True) + RMS_EPS) * gain


def _proj_body(x_ref, g1_ref, w_ref, gn_ref, c_ref, sp_ref, sm_ref, bd_ref,
               qa_ref, qb_ref, kb_ref, ka_ref, va_ref, vb_ref,
               cka_ref, cva_ref, ckb_ref, cvb_ref, *, cache_all, tiles_per_seq):
    h = _rms(x_ref[...], g1_ref[...]).astype(BF16)
    bd = bd_ref[...]

    def seg(lo, width):
        return jnp.dot(h, w_ref[:, lo:lo + width], preferred_element_type=F32)

    def head_norm(z, lo):
        width = z.shape[1]
        zz = (z * z).astype(BF16)
        parts = []
        for c in range(0, width, 256):
            w_ = min(256, width - c)
            parts.append(jnp.dot(zz[:, c:c + w_], bd[:w_, :w_], preferred_element_type=F32))
        ss = parts[0] if len(parts) == 1 else jnp.concatenate(parts, axis=1)
        return z * lax.rsqrt(ss * (1.0 / HEAD_DIM) + RMS_EPS) * gn_ref[:, lo:lo + width]

    def rope(z):
        c, sp, sm = c_ref[...], sp_ref[...], sm_ref[...]
        blocks = []
        for b in range(z.shape[1] // LANES):
            blk = z[:, b * LANES:(b + 1) * LANES]
            blocks.append(blk * c + pltpu.roll(blk, ROT_DIM // 2, 1) * sp
                          + pltpu.roll(blk, LANES - ROT_DIM // 2, 1) * sm)
        return blocks[0] if len(blocks) == 1 else jnp.concatenate(blocks, axis=1)

    qa_ref[...] = rope(head_norm(seg(QA0, A_Q), QA0)).astype(BF16)
    qb_ref[...] = head_norm(seg(QB0, B_W), QB0).astype(BF16)
    kb = head_norm(seg(KB0, B_W), KB0)
    kb_ref[...] = kb.astype(BF16)
    ka = rope(head_norm(seg(KA0, A_KV), KA0))
    ka_ref[...] = ka.astype(BF16)
    va = seg(VA0, A_KV)
    va_ref[...] = va.astype(BF16)
    vb = seg(VB0, B_W)
    vb_ref[...] = vb.astype(BF16)

    def write_cache():
        a_rows = cka_ref.shape[0]
        b_rows = ckb_ref.shape[0]
        cka_ref[...] = ka[ka.shape[0] - a_rows:, :]
        cva_ref[...] = va[va.shape[0] - a_rows:, :]
        ckb_ref[...] = kb[kb.shape[0] - b_rows:, :]
        cvb_ref[...] = vb[vb.shape[0] - b_rows:, :]

    if cache_all:
        write_cache()
    else:
        pl.when(pl.program_id(0) % tiles_per_seq == tiles_per_seq - 1)(write_cache)


def _project(x2d, params, tables, *, tm, tiles_per_seq, cache_all, n_seq):
    t_rows = x2d.shape[0]
    grid = (t_rows // tm,)
    row = lambda w: pl.BlockSpec((tm, w), lambda i: (i, 0))
    const = lambda a: pl.BlockSpec(a.shape, lambda i: (0,) * a.ndim)
    tab = pl.BlockSpec((tm, LANES), lambda i: (i % tiles_per_seq, 0))
    if cache_all:
        ca_spec, cb_spec = row(A_KV), row(B_W)
        ca_rows, cb_rows = t_rows, t_rows
    else:
        ca_spec = pl.BlockSpec((A_ROWS, A_KV), lambda i: (i // tiles_per_seq, 0))
        cb_spec = pl.BlockSpec((B_ROWS, B_W), lambda i: (i // tiles_per_seq, 0))
        ca_rows, cb_rows = n_seq * A_ROWS, n_seq * B_ROWS
    out_shape = (
        jax.ShapeDtypeStruct((t_rows, A_Q), BF16), jax.ShapeDtypeStruct((t_rows, B_W), BF16),
        jax.ShapeDtypeStruct((t_rows, B_W), BF16), jax.ShapeDtypeStruct((t_rows, A_KV), BF16),
        jax.ShapeDtypeStruct((t_rows, A_KV), BF16), jax.ShapeDtypeStruct((t_rows, B_W), BF16),
        jax.ShapeDtypeStruct((ca_rows, A_KV), F32), jax.ShapeDtypeStruct((ca_rows, A_KV), F32),
        jax.ShapeDtypeStruct((cb_rows, B_W), F32), jax.ShapeDtypeStruct((cb_rows, B_W), F32),
    )
    out_specs = (row(A_Q), row(B_W), row(B_W), row(A_KV), row(A_KV), row(B_W),
                 ca_spec, ca_spec, cb_spec, cb_spec)
    g1, w, gn, bd = params
    c, sp, sm = tables
    return pl.pallas_call(
        functools.partial(_proj_body, cache_all=cache_all, tiles_per_seq=tiles_per_seq),
        out_shape=out_shape, grid=grid,
        in_specs=[row(D_MODEL), const(g1), const(w), const(gn), tab, tab, tab, const(bd)],
        out_specs=out_specs,
        compiler_params=pltpu.CompilerParams(dimension_semantics=("arbitrary",),
                                             vmem_limit_bytes=VMEM_LIMIT),
        name="proj",
    )(x2d, g1, w, gn, c, sp, sm, bd)


def _stack_pairs(q, n_blocks):
    rows = q.shape[0]
    lo = lax.broadcasted_iota(jnp.int32, (rows, LANES), 1) < HEAD_DIM
    zero = jnp.zeros((rows, LANES), q.dtype)
    parts = []
    for b in range(n_blocks):
        q2 = q[:, b * LANES:(b + 1) * LANES]
        parts += [jnp.where(lo, q2, zero), jnp.where(lo, zero, q2)]
    return jnp.concatenate(parts, axis=0)


def _unstack_pairs(o, rows, n_blocks):
    lo = lax.broadcasted_iota(jnp.int32, (rows, LANES), 1) < HEAD_DIM
    blocks = []
    for b in range(n_blocks):
        top = o[(2 * b) * rows:(2 * b + 1) * rows]
        bot = o[(2 * b + 1) * rows:(2 * b + 2) * rows]
        blocks.append(jnp.where(lo, top, bot))
    return blocks[0] if n_blocks == 1 else jnp.concatenate(blocks, axis=1)


def _softmax_pv(qm, kband, vband, *, bias=None, sink=None, valid_from=None):
    s = lax.dot_general(qm, kband, (((1,), (1,)), ((), ())), preferred_element_type=F32)
    if bias is not None:
        s = s + bias
    if valid_from is not None:
        col = lax.broadcasted_iota(jnp.int32, s.shape, 1)
        s = jnp.where(col >= valid_from, s, MASK_VALUE)
    m = jnp.max(s, axis=1, keepdims=True)
    if sink is not None:
        m = jnp.maximum(m, sink)
    e = jnp.exp(s - m)
    l = jnp.sum(e, axis=1, keepdims=True)
    if sink is not None:
        l = l + jnp.exp(sink - m)
    return jnp.dot(e.astype(BF16), vband, preferred_element_type=F32) / l


def _route(logits):
    lane = lax.broadcasted_iota(jnp.int32, logits.shape, 1).astype(F32)
    neg = MASK_VALUE
    gmask = lane < float(N_GROUPS)
    gl = jnp.where(gmask, logits, neg)
    gmax = jnp.max(gl, axis=1, keepdims=True)
    gsum = jnp.sum(jnp.exp(gl - gmax), axis=1, keepdims=True)
    g_top_p = 1.0 / gsum
    g_top = jnp.min(jnp.where(gl == gmax, lane, float(LANES)), axis=1, keepdims=True)
    e_lo = float(ROUTE_EXPERT0) + float(EXPERTS_PER_GROUP) * g_top
    emask = (lane >= e_lo) & (lane < e_lo + float(EXPERTS_PER_GROUP))
    el = jnp.where(emask, logits, neg)
    v1 = jnp.max(el, axis=1, keepdims=True)
    i1 = jnp.min(jnp.where(el == v1, lane, float(LANES)), axis=1, keepdims=True)
    el2 = jnp.where(lane == i1, neg, el)
    v2 = jnp.max(el2, axis=1, keepdims=True)
    i2 = jnp.min(jnp.where(el2 == v2, lane, float(LANES)), axis=1, keepdims=True)
    t = jnp.exp(v2 - v1)
    den = 1.0 + t
    w1 = (1.0 / den) * g_top_p
    w2 = (t / den) * g_top_p
    first_lower = i1 < i2
    lo = jnp.minimum(i1, i2) - e_lo
    hi = jnp.maximum(i1, i2) - e_lo
    pair = lo * (float(2 * EXPERTS_PER_GROUP - 1) - lo) * 0.5 + (hi - lo - 1.0)
    bin_id = g_top * float(PAIRS_PER_GROUP) + pair
    w_lo = jnp.where(first_lower, w1, w2)
    w_hi = jnp.where(first_lower, w2, w1)
    return (jnp.where(lane == 0.0, w_lo, 0.0) + jnp.where(lane == 1.0, w_hi, 0.0)
            + jnp.where(lane == 2.0, bin_id, 0.0))


def _merge_route(o_all, x, wout_ref, g2_ref, wr_ref, br_ref, x1_ref, xm_ref):
    x1 = x + jnp.dot(o_all, wout_ref[...], preferred_element_type=F32)
    x1_ref[...] = x1
    xn = _rms(x1, g2_ref[...]).astype(BF16)
    logits = jnp.dot(xn, wr_ref[...], preferred_element_type=F32) + br_ref[...]
    bits = lax.bitcast_convert_type(xn.astype(F32), jnp.uint32)
    xm_ref[:, 0:HALF] = bits[:, HALF:] | lax.shift_right_logical(bits[:, :HALF], jnp.uint32(16))
    xm_ref[:, HALF:REC_W] = lax.bitcast_convert_type(_route(logits), jnp.uint32)


def _attn_prompt_body(qa_ref, qb_ref, kac_ref, kap_ref, vac_ref, vap_ref,
                      kbc_ref, kbp_ref, vbc_ref, vbp_ref,
                      x_ref, bias_ref, sink_ref, ga_ref, gb_ref, wout_ref, g2_ref, wr_ref, br_ref,
                      x1_ref, xm_ref,
                      ka_s, va_s, kb_s, vb_s, o_s):
    tm = qa_ref.shape[0]
    ka_s[0:A_ROWS] = kap_ref[...]
    ka_s[A_ROWS:A_ROWS + tm] = kac_ref[...]
    va_s[0:A_ROWS] = vap_ref[...]
    va_s[A_ROWS:A_ROWS + tm] = vac_ref[...]
    kb_s[0:B_ROWS] = kbp_ref[...]
    kb_s[B_ROWS:B_ROWS + tm] = kbc_ref[...]
    vb_s[0:B_ROWS] = vbp_ref[...]
    vb_s[B_ROWS:B_ROWS + tm] = vbc_ref[...]
    sink = sink_ref[:, 0:1]

    def chunk(i, masked):
        r0 = pl.multiple_of(i * CHUNK, CHUNK)
        qm = _stack_pairs(qa_ref[pl.ds(r0, CHUNK), :], A_Q // LANES)
        o = _softmax_pv(qm, ka_s[pl.ds(r0, A_ROWS + CHUNK), :], va_s[pl.ds(r0, A_ROWS + CHUNK), :],
                        sink=sink, valid_from=(A_ROWS - CHUNK * i) if masked else None)
        oa = _unstack_pairs(o, CHUNK, A_Q // LANES)
        o_s[pl.ds(r0, CHUNK), 0:A_Q] = _rms(oa, ga_ref[...]).astype(BF16)
        qb_c = qb_ref[pl.ds(r0, CHUNK), :]
        blocks = []
        for p in range(B_W // LANES):
            qm = _stack_pairs(qb_c[:, p * LANES:(p + 1) * LANES], 1)
            o = _softmax_pv(qm, kb_s[pl.ds(r0, B_ROWS + CHUNK), p * LANES:(p + 1) * LANES],
                            vb_s[pl.ds(r0, B_ROWS + CHUNK), p * LANES:(p + 1) * LANES],
                            bias=bias_ref[p], valid_from=(B_ROWS - CHUNK * i) if masked else None)
            blocks.append(_unstack_pairs(o, CHUNK, 1))
        ob = jnp.concatenate(blocks, axis=1)
        o_s[pl.ds(r0, CHUNK), A_Q:A_Q + B_W] = _rms(ob, gb_ref[...]).astype(BF16)

    def run(masked):
        def body(i, carry):
            chunk(i, masked)
            return carry
        lax.fori_loop(0, tm // CHUNK, body, 0)

    first = pl.program_id(1) == 0
    pl.when(first)(lambda: run(True))
    pl.when(jnp.logical_not(first))(lambda: run(False))

    _merge_route(o_s[...], x_ref[...], wout_ref, g2_ref, wr_ref, br_ref, x1_ref, xm_ref)


def _attn_prompt(proj_out, x2d, consts, *, n_batch, tiles_per_seq):
    qa, qb, kb, ka, va, vb = proj_out
    bias2, sink_tab, ga, gb, wout, g2, wr, br = consts
    tm = ROW_TILE
    t_rows = x2d.shape[0]
    tile = lambda n, j: n * tiles_per_seq + j
    cur = lambda w: pl.BlockSpec((tm, w), lambda n, j: (tile(n, j), 0))
    prev_a = pl.BlockSpec((A_ROWS, A_KV),
                          lambda n, j: (jnp.maximum(tile(n, j) * (tm // A_ROWS) - 1, 0), 0))
    prev_b = pl.BlockSpec((B_ROWS, B_W), lambda n, j: (jnp.maximum(tile(n, j) - 1, 0), 0))
    const = lambda a: pl.BlockSpec(a.shape, lambda n, j: (0,) * a.ndim)
    return pl.pallas_call(
        _attn_prompt_body,
        out_shape=(jax.ShapeDtypeStruct((t_rows, D_MODEL), F32),
                   jax.ShapeDtypeStruct((t_rows, REC_W), jnp.uint32)),
        grid=(n_batch, tiles_per_seq),
        in_specs=[cur(A_Q), cur(B_W), cur(A_KV), prev_a, cur(A_KV), prev_a,
                  cur(B_W), prev_b, cur(B_W), prev_b,
                  cur(D_MODEL), const(bias2), const(sink_tab), const(ga), const(gb),
                  const(wout), const(g2), const(wr), const(br)],
        out_specs=(cur(D_MODEL), cur(REC_W)),
        scratch_shapes=[pltpu.VMEM((A_ROWS + tm, A_KV), BF16), pltpu.VMEM((A_ROWS + tm, A_KV), BF16),
                        pltpu.VMEM((B_ROWS + tm, B_W), BF16), pltpu.VMEM((B_ROWS + tm, B_W), BF16),
                        pltpu.VMEM((tm, D_MODEL), BF16)],
        compiler_params=pltpu.CompilerParams(dimension_semantics=("arbitrary", "arbitrary"),
                                             vmem_limit_bytes=VMEM_LIMIT),
        name="attn_prompt",
    )(qa, qb, ka, ka, va, va, kb, kb, vb, vb, x2d, bias2, sink_tab, ga, gb, wout, g2, wr, br)


def _attn_sample_body(qa_ref, qb_ref, kan_ref, van_ref, kbn_ref, vbn_ref,
                      cak_ref, cav_ref, cbk_ref, cbv_ref,
                      x_ref, bias_ref, sink_ref, ga_ref, gb_ref, wout_ref, g2_ref, wr_ref, br_ref,
                      x1_ref, xm_ref,
                      ka_s, va_s, kb_s, vb_s, o_s, *, seq):
    n_elem = cak_ref.shape[0]
    wa = cak_ref.shape[1]
    wb = cbk_ref.shape[1]
    sink = sink_ref[:, 0:1]

    def element(e, carry):
        r0 = pl.multiple_of(e * seq, seq)
        ka_s[0:wa] = cak_ref[e].astype(BF16)
        ka_s[wa:wa + seq] = kan_ref[pl.ds(r0, seq), :]
        va_s[0:wa] = cav_ref[e].astype(BF16)
        va_s[wa:wa + seq] = van_ref[pl.ds(r0, seq), :]
        kb_s[0:wb] = cbk_ref[e].astype(BF16)
        kb_s[wb:wb + seq] = kbn_ref[pl.ds(r0, seq), :]
        vb_s[0:wb] = cbv_ref[e].astype(BF16)
        vb_s[wb:wb + seq] = vbn_ref[pl.ds(r0, seq), :]

        qm = _stack_pairs(qa_ref[pl.ds(r0, seq), :], A_Q // LANES)
        o = _softmax_pv(qm, ka_s[...], va_s[...], sink=sink)
        oa = _unstack_pairs(o, seq, A_Q // LANES)
        o_s[pl.ds(r0, seq), 0:A_Q] = _rms(oa, ga_ref[...]).astype(BF16)

        qb_c = qb_ref[pl.ds(r0, seq), :]
        blocks = []
        for p in range(B_W // LANES):
            qm = _stack_pairs(qb_c[:, p * LANES:(p + 1) * LANES], 1)
            o = _softmax_pv(qm, kb_s[:, p * LANES:(p + 1) * LANES],
                            vb_s[:, p * LANES:(p + 1) * LANES], bias=bias_ref[p])
            blocks.append(_unstack_pairs(o, seq, 1))
        ob = jnp.concatenate(blocks, axis=1)
        o_s[pl.ds(r0, seq), A_Q:A_Q + B_W] = _rms(ob, gb_ref[...]).astype(BF16)
        return carry

    lax.fori_loop(0, n_elem, element, 0)
    _merge_route(o_s[...], x_ref[...], wout_ref, g2_ref, wr_ref, br_ref, x1_ref, xm_ref)


def _attn_sample(proj_out, caches, x2d, consts, *, n_batch, seq, elems_per_step):
    qa, qb, kb, ka, va, vb = proj_out
    cak, cav, cbk, cbv = caches
    bias_s, sink_tab, ga, gb, wout, g2, wr, br = consts
    wa, wb = cak.shape[1], cbk.shape[1]
    tm = elems_per_step * seq
    t_rows = x2d.shape[0]
    row = lambda w: pl.BlockSpec((tm, w), lambda i: (i, 0))
    cache = lambda a: pl.BlockSpec((elems_per_step,) + a.shape[1:], lambda i: (i, 0, 0))
    const = lambda a: pl.BlockSpec(a.shape, lambda i: (0,) * a.ndim)
    operands = (qa, qb, ka, va, kb, vb, cak, cav, cbk, cbv, x2d, bias_s, sink_tab, ga, gb, wout, g2,
                wr, br)
    return pl.pallas_call(
        functools.partial(_attn_sample_body, seq=seq),
        out_shape=(jax.ShapeDtypeStruct((t_rows, D_MODEL), F32),
                   jax.ShapeDtypeStruct((t_rows, REC_W), jnp.uint32)),
        grid=(n_batch // elems_per_step,),
        in_specs=[row(A_Q), row(B_W), row(A_KV), row(A_KV), row(B_W), row(B_W),
                  cache(cak), cache(cav), cache(cbk), cache(cbv),
                  row(D_MODEL), const(bias_s), const(sink_tab), const(ga), const(gb),
                  const(wout), const(g2), const(wr), const(br)],
        out_specs=(row(D_MODEL), row(REC_W)),
        scratch_shapes=[pltpu.VMEM((wa + seq, A_KV), BF16), pltpu.VMEM((wa + seq, A_KV), BF16),
                        pltpu.VMEM((wb + seq, B_W), BF16), pltpu.VMEM((wb + seq, B_W), BF16),
                        pltpu.VMEM((tm, D_MODEL), BF16)],
        compiler_params=pltpu.CompilerParams(dimension_semantics=("arbitrary",),
                                             vmem_limit_bytes=VMEM_LIMIT),
        name="attn_sample",
    )(*operands)


def _positions_body(recp_ref, recs_ref, dest_ref, counts_ref, cnt_s, carry_s, off_s, *, prompt_tiles):
    ph, i = pl.program_id(0), pl.program_id(1)
    rec = lax.bitcast_convert_type(jnp.where(i < prompt_tiles, recp_ref[...], recs_ref[...]), F32)
    rows = rec.shape[0]
    lane = lax.broadcasted_iota(jnp.int32, rec.shape, 1).astype(F32)
    onehot = lane == rec[:, 2:3]
    oh = jnp.where(onehot, 1.0, 0.0)
    col_count = jnp.sum(oh, axis=0, keepdims=True)

    @pl.when((ph == 0) & (i == 0))
    def _():
        cnt_s[...] = jnp.zeros_like(cnt_s)

    @pl.when(ph == 0)
    def _():
        cnt_s[...] += col_count

    @pl.when((ph == 1) & (i == 0))
    def _():
        c = jnp.broadcast_to(cnt_s[...], (8, LANES))
        c_hi = jnp.floor(c * (1.0 / 256.0))
        c_lo = c - 256.0 * c_hi
        r = lax.broadcasted_iota(jnp.int32, (LANES, LANES), 0)
        q = lax.broadcasted_iota(jnp.int32, (LANES, LANES), 1)
        upper = jnp.where(r < q, 1.0, 0.0).astype(BF16)
        off = (256.0 * jnp.dot(c_hi.astype(BF16), upper, preferred_element_type=F32)
               + jnp.dot(c_lo.astype(BF16), upper, preferred_element_type=F32))
        off_s[...] = off[0:1]
        carry_s[...] = jnp.zeros_like(carry_s)
        counts_ref[...] = c

    @pl.when(ph == 1)
    def _():
        r = lax.broadcasted_iota(jnp.int32, (rows, rows), 0)
        q = lax.broadcasted_iota(jnp.int32, (rows, rows), 1)
        lower = jnp.where(q < r, 1.0, 0.0).astype(BF16)
        rank = jnp.dot(lower, oh.astype(BF16), preferred_element_type=F32)
        tot = rank + carry_s[...] + off_s[...]
        d = jnp.sum(jnp.where(onehot, tot, 0.0), axis=1, keepdims=True)
        dest_ref[...] = jnp.broadcast_to(d, dest_ref.shape)
        carry_s[...] += col_count


def _prompt_tile(prompt_tiles):
    return lambda i: jnp.minimum(i, prompt_tiles - 1)


def _sample_tile(prompt_tiles):
    return lambda i: jnp.maximum(i - prompt_tiles, 0)


def _positions(xm_p, xm_s):
    prompt_tiles = xm_p.shape[0] // ROW_TILE
    n_tiles = prompt_tiles + xm_s.shape[0] // ROW_TILE
    t_all = n_tiles * ROW_TILE
    rec_block = HALF // LANES
    tp, ts = _prompt_tile(prompt_tiles), _sample_tile(prompt_tiles)
    return pl.pallas_call(
        functools.partial(_positions_body, prompt_tiles=prompt_tiles),
        out_shape=(jax.ShapeDtypeStruct((t_all, LANES), F32), jax.ShapeDtypeStruct((8, LANES), F32)),
        grid=(2, n_tiles),
        in_specs=[pl.BlockSpec((ROW_TILE, LANES), lambda ph, i: (tp(i), rec_block)),
                  pl.BlockSpec((ROW_TILE, LANES), lambda ph, i: (ts(i), rec_block))],
        out_specs=(pl.BlockSpec((ROW_TILE, LANES), lambda ph, i: (i * ph, 0)),
                   pl.BlockSpec((8, LANES), lambda ph, i: (0, 0))),
        scratch_shapes=[pltpu.VMEM((1, LANES), F32), pltpu.VMEM((1, LANES), F32),
                        pltpu.VMEM((1, LANES), F32)],
        compiler_params=pltpu.CompilerParams(dimension_semantics=("arbitrary", "arbitrary")),
        name="positions",
    )(xm_p, xm_s)


def _visit_plan(counts, n_tiles):
    tile = MOE_ROW_TILE
    n_visits = n_tiles + N_BINS
    pairs = np.array(PAIRS, np.int32)
    bin_lo = np.concatenate([g * EXPERTS_PER_GROUP + pairs[:, 0] for g in range(N_GROUPS)])
    bin_hi = np.concatenate([g * EXPERTS_PER_GROUP + pairs[:, 1] for g in range(N_GROUPS)])
    end = jnp.cumsum(counts)
    off = end - counts
    first = off // tile
    last = jnp.maximum(end - 1, 0) // tile
    nv = jnp.where(counts > 0, last - first + 1, 0)
    vend = jnp.cumsum(nv)
    vstart = vend - nv
    total = vend[-1]
    v = jnp.arange(n_visits, dtype=jnp.int32)
    valid = v < total
    b = jnp.minimum(jnp.searchsorted(vend, jnp.minimum(v, total - 1), side="right"), N_BINS - 1)
    b = b.astype(jnp.int32)
    t = jnp.where(valid, first[b] + (v - vstart[b]), n_tiles - 1)
    r0 = jnp.where(valid, jnp.maximum(off[b], t * tile) - t * tile, 0)
    r1 = jnp.where(valid, jnp.minimum(end[b], (t + 1) * tile) - t * tile, 0)
    t_prev = jnp.concatenate([jnp.full((1,), -1, jnp.int32), t[:-1]])
    init = (valid & (t != t_prev)).astype(jnp.int32)
    i32 = lambda a: a.astype(jnp.int32)
    return (i32(t), i32(jnp.asarray(bin_lo)[b]), i32(jnp.asarray(bin_hi)[b]), i32(r0), i32(r1), init)


def _drain(copy, n):
    def body(_, carry):
        for _ in range(DRAIN_UNROLL):
            copy.wait()
        return carry
    lax.fori_loop(0, n // DRAIN_UNROLL, body, 0)


def _scatter_body(dest_ref, xmp_ref, xms_ref, xs_ref, sem, *, prompt_tiles):
    i = pl.program_id(0)
    rows = xmp_ref.shape[0]
    base = i * rows

    def send(src_ref):
        def issue(r, carry):
            d = dest_ref[base + r]
            pltpu.make_async_copy(src_ref.at[pl.ds(r, 1), :], xs_ref.at[pl.ds(d, 1), :], sem).start()
            return carry
        lax.fori_loop(0, rows, issue, 0, unroll=8)
        _drain(pltpu.make_async_copy(src_ref.at[pl.ds(0, 1), :], xs_ref.at[pl.ds(0, 1), :], sem), rows)

    pl.when(i < prompt_tiles)(lambda: send(xmp_ref))
    pl.when(i >= prompt_tiles)(lambda: send(xms_ref))


def _scatter_rows(dest, xm_p, xm_s):
    prompt_tiles = xm_p.shape[0] // ROW_TILE
    n_tiles = prompt_tiles + xm_s.shape[0] // ROW_TILE
    tp, ts = _prompt_tile(prompt_tiles), _sample_tile(prompt_tiles)
    return pl.pallas_call(
        functools.partial(_scatter_body, prompt_tiles=prompt_tiles),
        out_shape=jax.ShapeDtypeStruct((n_tiles * ROW_TILE, REC_W), xm_p.dtype),
        grid_spec=pltpu.PrefetchScalarGridSpec(
            num_scalar_prefetch=1, grid=(n_tiles,),
            in_specs=[pl.BlockSpec((ROW_TILE, REC_W), lambda i, dest: (tp(i), 0)),
                      pl.BlockSpec((ROW_TILE, REC_W), lambda i, dest: (ts(i), 0))],
            out_specs=pl.BlockSpec(memory_space=pl.ANY),
            scratch_shapes=[pltpu.SemaphoreType.DMA(())]),
        compiler_params=pltpu.CompilerParams(dimension_semantics=("arbitrary",)),
        name="scatter_rows",
    )(dest, xm_p, xm_s)


def _moe_body(vt_ref, vlo_ref, vhi_ref, r0_ref, r1_ref, init_ref,
              xs_ref, wgl_ref, wul_ref, wdl_ref, wgh_ref, wuh_ref, wdh_ref, o_ref):
    v = pl.program_id(0)
    r0, r1 = r0_ref[v], r1_ref[v]

    @pl.when(r1 > r0)
    def _():
        words = xs_ref[:, 0:HALF]
        x_lo = lax.bitcast_convert_type(lax.shift_left(words, jnp.uint32(16)), F32)
        x_hi = lax.bitcast_convert_type(words & jnp.uint32(0xFFFF0000), F32)
        x = jnp.concatenate([x_lo, x_hi], axis=1).astype(BF16)
        rec = lax.bitcast_convert_type(xs_ref[:, HALF:REC_W], F32)

        def expert(wg_ref, wu_ref, wd_ref, w):
            a = jnp.dot(x, wg_ref[...], preferred_element_type=F32)
            b = jnp.dot(x, wu_ref[...], preferred_element_type=F32)
            hdn = (jax.nn.silu(a) * b * w).astype(BF16)
            return jnp.dot(hdn, wd_ref[...], preferred_element_type=F32)

        out = (expert(wgl_ref, wul_ref, wdl_ref, rec[:, 0:1])
               + expert(wgh_ref, wuh_ref, wdh_ref, rec[:, 1:2]))
        row = lax.broadcasted_iota(jnp.int32, (out.shape[0], 1), 0)
        mine = (row >= r0) & (row < r1)

        @pl.when(init_ref[v] == 1)
        def _():
            o_ref[...] = jnp.where(mine, out, 0.0)

        @pl.when(init_ref[v] == 0)
        def _():
            o_ref[...] = jnp.where(mine, out, o_ref[...])


def _moe_visits(plan, xs, wg, wu, wd):
    t_all = xs.shape[0]
    n_visits = plan[0].shape[0]
    tile = MOE_ROW_TILE
    rows = lambda w: pl.BlockSpec((tile, w), lambda v, vt, vlo, vhi, r0, r1, init: (vt[v], 0))
    w_lo = lambda a: pl.BlockSpec((None,) + a.shape[1:], lambda v, vt, vlo, vhi, r0, r1, init: (vlo[v], 0, 0))
    w_hi = lambda a: pl.BlockSpec((None,) + a.shape[1:], lambda v, vt, vlo, vhi, r0, r1, init: (vhi[v], 0, 0))
    return pl.pallas_call(
        _moe_body,
        out_shape=jax.ShapeDtypeStruct((t_all, D_MODEL), F32),
        grid_spec=pltpu.PrefetchScalarGridSpec(
            num_scalar_prefetch=6, grid=(n_visits,),
            in_specs=[rows(REC_W), w_lo(wg), w_lo(wu), w_lo(wd), w_hi(wg), w_hi(wu), w_hi(wd)],
            out_specs=rows(D_MODEL)),
        compiler_params=pltpu.CompilerParams(dimension_semantics=("arbitrary",),
                                             vmem_limit_bytes=VMEM_LIMIT),
        name="moe_visits",
    )(*plan, xs, wg, wu, wd, wg, wu, wd)


def _combine_body(dest_ref, x1p_ref, x1s_ref, o_ref, yp_ref, ys_ref, buf, sem, *, prompt_tiles):
    i = pl.program_id(0)
    n = pl.num_programs(0)
    rows = x1p_ref.shape[0]

    def row_copy(tile, r, slot):
        d = dest_ref[tile * rows + r]
        return pltpu.make_async_copy(o_ref.at[pl.ds(d, 1), :], buf.at[slot, pl.ds(r, 1), :], sem.at[slot])

    def issue(tile, slot):
        def body(r, carry):
            row_copy(tile, r, slot).start()
            return carry
        lax.fori_loop(0, rows, body, 0, unroll=8)

    slot = i % 2
    pl.when(i == 0)(lambda: issue(0, 0))
    pl.when(i + 1 < n)(lambda: issue(i + 1, 1 - slot))
    _drain(pltpu.make_async_copy(o_ref.at[pl.ds(0, 1), :], buf.at[slot, pl.ds(0, 1), :], sem.at[slot]), rows)
    @pl.when(i < prompt_tiles)
    def _():
        yp_ref[...] = x1p_ref[...] + buf[slot]

    @pl.when(i >= prompt_tiles)
    def _():
        ys_ref[...] = x1s_ref[...] + buf[slot]


def _combine(dest, x1_p, x1_s, o_sorted):
    prompt_tiles = x1_p.shape[0] // ROW_TILE
    n_tiles = prompt_tiles + x1_s.shape[0] // ROW_TILE
    tp, ts = _prompt_tile(prompt_tiles), _sample_tile(prompt_tiles)
    p_spec = pl.BlockSpec((ROW_TILE, D_MODEL), lambda i, dest: (tp(i), 0))
    s_spec = pl.BlockSpec((ROW_TILE, D_MODEL), lambda i, dest: (ts(i), 0))
    return pl.pallas_call(
        functools.partial(_combine_body, prompt_tiles=prompt_tiles),
        out_shape=(jax.ShapeDtypeStruct(x1_p.shape, F32), jax.ShapeDtypeStruct(x1_s.shape, F32)),
        grid_spec=pltpu.PrefetchScalarGridSpec(
            num_scalar_prefetch=1, grid=(n_tiles,),
            in_specs=[p_spec, s_spec, pl.BlockSpec(memory_space=pl.ANY)],
            out_specs=(p_spec, s_spec),
            scratch_shapes=[pltpu.VMEM((2, ROW_TILE, D_MODEL), F32), pltpu.SemaphoreType.DMA((2,))]),
        compiler_params=pltpu.CompilerParams(dimension_semantics=("arbitrary",),
                                             vmem_limit_bytes=VMEM_LIMIT),
        name="combine",
    )(dest, x1_p, x1_s, o_sorted)


def _rope_tables(pos):
    half = ROT_DIM // 2
    inv_freq = ROPE_THETA ** (-jnp.arange(half, dtype=F32) / half)
    ang = pos.astype(F32)[:, None] * inv_freq[None, :]
    cos, sin = jnp.cos(ang), jnp.sin(ang)
    n = pos.shape[0]
    rest = HEAD_DIM - ROT_DIM
    c = jnp.concatenate([cos, cos, jnp.ones((n, rest), F32)], axis=1)
    sp = jnp.concatenate([jnp.zeros((n, half), F32), sin, jnp.zeros((n, rest), F32)], axis=1)
    sm = jnp.concatenate([-sin, jnp.zeros((n, half + rest), F32)], axis=1)
    rep = LANES // HEAD_DIM
    return tuple(jnp.tile(t, (1, rep)) for t in (c, sp, sm))


def _pair_bias(rel_bias, rows, cols):
    k = np.arange(cols + rows - 1)
    idx = np.clip(k - (rows - 1) - B_ROWS, -REL_CLIP, REL_CLIP) + REL_CLIP
    diag = rel_bias.astype(F32)[:, idx]
    full = jnp.stack([diag[:, rows - 1 - i:rows - 1 - i + cols] for i in range(rows)], axis=1)
    return full.reshape(B_HEADS // 2, 2 * rows, cols)


def _sink_table(sinks, rows):
    order = jnp.asarray(A_HEAD_ORDER)
    per_row = jnp.repeat(sinks.astype(F32)[order], rows)
    return jnp.broadcast_to(per_row[:, None], (per_row.shape[0], LANES))


def kernel(x_prompt, x_sample, cache_a_k, cache_a_v, cache_b_k, cache_b_v, norm1, w_in, q_norm_a,
           k_norm_a, q_norm_b, k_norm_b, sinks_a, rel_bias_b, out_norm_a, out_norm_b, w_out, norm2,
           w_router_group, b_router_group, w_router_expert, b_router_expert, w_gate, w_up, w_down):
    depth = norm1.shape[0]
    assert depth == 1, "single layer"
    n_b, seq, _ = x_prompt.shape
    n_s, seq_s, _ = x_sample.shape
    assert seq % ROW_TILE == 0 and ROW_TILE % seq_s == 0 and (n_s * seq_s) % ROW_TILE == 0
    order = jnp.asarray(A_HEAD_ORDER)

    w = w_in[0]
    qa_w = w[:, 0:A_Q].reshape(D_MODEL, A_HEADS, HEAD_DIM)[:, order].reshape(D_MODEL, A_Q)
    o_qb = A_Q + 2 * A_KV
    w_perm = jnp.concatenate([qa_w, w[:, o_qb:o_qb + B_W], w[:, o_qb + B_W:o_qb + 2 * B_W],
                              w[:, A_Q:A_Q + A_KV], w[:, A_Q + A_KV:A_Q + 2 * A_KV],
                              w[:, o_qb + 2 * B_W:]], axis=1).astype(BF16)
    q_scale = HEAD_DIM ** -0.5
    gn = jnp.concatenate([jnp.tile(q_norm_a[0], A_HEADS) * q_scale, jnp.tile(q_norm_b[0], B_HEADS) * q_scale,
                          jnp.tile(k_norm_b[0], B_HEADS), jnp.tile(k_norm_a[0], A_KV_HEADS)])[None, :].astype(F32)
    g1 = norm1[0][None, :].astype(F32)
    r = np.arange(256)
    bd = jnp.asarray((r[:, None] // HEAD_DIM) == (r[None, :] // HEAD_DIM), dtype=BF16)
    proj_params = (g1, w_perm, gn, bd)

    ga = out_norm_a[0].reshape(A_HEADS, HEAD_DIM)[order].reshape(1, A_Q).astype(F32)
    gb = out_norm_b[0][None, :].astype(F32)
    wo = w_out[0]
    wout = jnp.concatenate([wo[:A_Q].reshape(A_HEADS, HEAD_DIM, D_MODEL)[order].reshape(A_Q, D_MODEL),
                            wo[A_Q:]], axis=0).astype(BF16)
    g2 = norm2[0][None, :].astype(F32)
    n_route = N_GROUPS + N_EXPERTS
    wr = jnp.concatenate([w_router_group[0],
                          jnp.transpose(w_router_expert[0], (1, 0, 2)).reshape(D_MODEL, N_EXPERTS),
                          jnp.zeros((D_MODEL, LANES - n_route), F32)], axis=1).astype(BF16)
    br = jnp.concatenate([b_router_group[0], b_router_expert[0].reshape(-1),
                          jnp.zeros((LANES - n_route,), F32)])[None, :].astype(F32)
    wg = w_gate[0].reshape(N_EXPERTS, D_MODEL, EXPERT_FF).astype(BF16)
    wu = w_up[0].reshape(N_EXPERTS, D_MODEL, EXPERT_FF).astype(BF16)
    wd = w_down[0].reshape(N_EXPERTS, EXPERT_FF, D_MODEL).astype(BF16)

    xp = x_prompt.reshape(n_b * seq, D_MODEL)
    tps = seq // ROW_TILE
    outs = _project(xp, proj_params, _rope_tables(jnp.arange(seq)), tm=ROW_TILE, tiles_per_seq=tps,
                    cache_all=False, n_seq=n_b)
    ak_p, av_p, bk_p, bv_p = outs[6:]
    consts = (_pair_bias(rel_bias_b[0], CHUNK, B_ROWS + CHUNK), _sink_table(sinks_a[0], CHUNK),
              ga, gb, wout, g2, wr, br)
    x1_p, xm_p = _attn_prompt(outs[:6], xp, consts, n_batch=n_b, tiles_per_seq=tps)

    xs = x_sample.reshape(n_s * seq_s, D_MODEL)
    pos_s = jnp.tile(PAST_LEN + jnp.arange(seq_s), ROW_TILE // seq_s)
    outs = _project(xs, proj_params, _rope_tables(pos_s), tm=ROW_TILE, tiles_per_seq=1,
                    cache_all=True, n_seq=n_s)
    ak_s, av_s, bk_s, bv_s = outs[6:]
    wa, wb = cache_a_k.shape[2], cache_b_k.shape[2]
    caches = (cache_a_k[0].reshape(n_s, wa, A_KV), cache_a_v[0].reshape(n_s, wa, A_KV),
              cache_b_k[0].reshape(n_s, wb, B_W), cache_b_v[0].reshape(n_s, wb, B_W))
    consts = (_pair_bias(rel_bias_b[0], seq_s, wb + seq_s), _sink_table(sinks_a[0], seq_s),
              ga, gb, wout, g2, wr, br)
    x1_s, xm_s = _attn_sample(outs[:6], caches, xs, consts, n_batch=n_s, seq=seq_s, elems_per_step=4)

    dest_f, counts_f = _positions(xm_p, xm_s)
    dest = dest_f[:, 0].astype(jnp.int32)
    counts = counts_f[0, :N_BINS].astype(jnp.int32)
    plan = _visit_plan(counts, dest.shape[0] // MOE_ROW_TILE)
    o_sorted = _moe_visits(plan, _scatter_rows(dest, xm_p, xm_s), wg, wu, wd)
    y_p, y_s = _combine(dest, x1_p, x1_s, o_sorted)
    y_p = y_p.reshape(n_b, seq, D_MODEL)
    y_s = y_s.reshape(n_s, seq_s, D_MODEL)

    return (y_p, y_s,
            ak_p.reshape(1, n_b, A_ROWS, A_KV_HEADS, HEAD_DIM), av_p.reshape(1, n_b, A_ROWS, A_KV_HEADS, HEAD_DIM),
            bk_p.reshape(1, n_b, B_ROWS, B_HEADS, HEAD_DIM), bv_p.reshape(1, n_b, B_ROWS, B_HEADS, HEAD_DIM),
            ak_s.reshape(1, n_s, seq_s, A_KV_HEADS, HEAD_DIM), av_s.reshape(1, n_s, seq_s, A_KV_HEADS, HEAD_DIM),
            bk_s.reshape(1, n_s, seq_s, B_HEADS, HEAD_DIM), bv_s.reshape(1, n_s, seq_s, B_HEADS, HEAD_DIM))
```
